```python
import math
import jax, jax.numpy as jnp
from jax import lax
import numpy as np

D_MODEL = 1024
BATCH = 1
SEQ = 16384
DEPTH = 4

CHUNK = 64
D_MIX = D_MODEL
D_LRU = D_MIX // 2
LRU_BLOCKS = 8
LRU_BLOCK = D_LRU // LRU_BLOCKS
CONV_WIDTH = 4
LRU_C = 8.0
GLA_HEADS = 4
GLA_DV = (D_MIX - D_LRU) // GLA_HEADS
GLA_DK = GLA_DV // 2
GLA_K = GLA_HEADS * GLA_DK
GLA_V = GLA_HEADS * GLA_DV
GATE_RANK = 16
GATE_TAU = 16.0
D_FF = ((8 * D_MODEL // 3 + 255) // 256) * 256
EPS = 1e-6

SPLIT_SIZES = (D_LRU, D_LRU, GLA_K, GLA_K, GLA_V, GLA_V, GATE_RANK)
P_IN = sum(SPLIT_SIZES)
SPLIT_IDX = tuple(int(v) for v in np.cumsum(SPLIT_SIZES)[:-1])

kernel_name = "hymba_style_rglru_gla_hybrid"


def rms_norm(x, gain):
    xf = x.astype(jnp.float32)
    y = xf * lax.rsqrt(jnp.mean(xf * xf, axis=-1, keepdims=True) + EPS)
    return (y * gain.astype(jnp.float32)).astype(x.dtype)


def causal_depthwise_conv(x, w, b):
    seq = x.shape[1]
    xp = jnp.pad(x, ((0, 0), (CONV_WIDTH - 1, 0), (0, 0)))
    y = xp[:, 0:seq, :] * w[0]
    for j in range(1, CONV_WIDTH):
        y = y + xp[:, j:j + seq, :] * w[j]
    return y + b


def block_diag_linear(x, w, b):
    xb = x.reshape(x.shape[:-1] + (LRU_BLOCKS, LRU_BLOCK))
    y = jnp.einsum("bsni,nij->bsnj", xb, w.astype(x.dtype))
    return y.reshape(x.shape) + b.astype(x.dtype)


def rg_lru(x, w_a, b_a, w_i, b_i, lam):
    xf = x.astype(jnp.float32)
    r = jax.nn.sigmoid(block_diag_linear(xf, w_a, b_a))
    i = jax.nn.sigmoid(block_diag_linear(xf, w_i, b_i))
    log_a = -LRU_C * r * jax.nn.softplus(-lam.astype(jnp.float32))
    a = jnp.exp(log_a)
    u = jnp.sqrt(-jnp.expm1(2.0 * log_a)) * (i * xf)

    def step(h, au):
        a_t, u_t = au
        h = a_t * h + u_t
        return h, h

    h0 = jnp.zeros((xf.shape[0], xf.shape[2]), jnp.float32)
    _, hs = lax.scan(step, h0, (a.swapaxes(0, 1), u.swapaxes(0, 1)))
    return hs.swapaxes(0, 1).astype(x.dtype)


def gla_chunk_causal(q, k, v, log_alpha):
    bsz, seq = q.shape[:2]
    nc = seq // CHUNK

    def rs(t):
        return t.astype(jnp.float32).reshape(bsz, nc, CHUNK, GLA_HEADS, t.shape[-1])

    q, k, v, la = rs(q), rs(k), rs(v), rs(log_alpha)
    bcum = jnp.cumsum(la, axis=2)
    b_end = bcum[:, :, -1:]
    k_dec = k * jnp.exp(b_end - bcum)
    q_dec = q * jnp.exp(b_end)
    scores = jnp.einsum("bnshd,bnthd->bnhst", q, k_dec)
    o_intra = jnp.einsum("bnhst,bnthv->bnshv", scores, v)
    updates = jnp.einsum("bnthd,bnthv->bnhdv", k_dec, v)
    decay = jnp.exp(b_end[:, :, 0])

    def step(state, inp):
        d, upd = inp
        return d[..., None] * state + upd, state

    s0 = jnp.zeros((bsz, GLA_HEADS, GLA_DK, GLA_DV), jnp.float32)
    _, s_prev = lax.scan(step, s0, (decay.swapaxes(0, 1), updates.swapaxes(0, 1)))
    s_prev = s_prev.swapaxes(0, 1)
    o_inter = jnp.einsum("bnshd,bnhdv->bnshv", q_dec, s_prev)
    return (o_intra + o_inter).reshape(bsz, seq, GLA_HEADS, GLA_DV)


def setup_inputs(seed: int = 0) -> dict:
    key = jax.random.key(seed)
    ks = jax.random.split(key, 24)
    f32 = jnp.float32

    def nrm(k, shape, scale):
        return jax.random.normal(k, shape, f32) * scale

    def gain(k, shape):
        return 1.0 + 0.02 * jax.random.normal(k, shape, f32)

    u = jax.random.uniform(ks[10], (DEPTH, D_LRU), f32, 0.9, 0.999)
    s = u ** (1.0 / LRU_C)
    lam = jnp.log(s) - jnp.log1p(-s)
    return {
        "x": jax.random.normal(ks[0], (BATCH, SEQ, D_MODEL), f32),
        "norm1": gain(ks[1], (DEPTH, D_MODEL)),
        "w_in": nrm(ks[2], (DEPTH, D_MODEL, P_IN), D_MODEL ** -0.5),
        "conv_w": nrm(ks[3], (DEPTH, CONV_WIDTH, D_LRU), CONV_WIDTH ** -0.5),
        "conv_b": nrm(ks[4], (DEPTH, D_LRU), 0.02),
        "lru_wa": nrm(ks[5], (DEPTH, LRU_BLOCKS, LRU_BLOCK, LRU_BLOCK), LRU_BLOCK ** -0.5),
        "lru_ba": nrm(ks[6], (DEPTH, D_LRU), 0.02),
        "lru_wi": nrm(ks[7], (DEPTH, LRU_BLOCKS, LRU_BLOCK, LRU_BLOCK), LRU_BLOCK ** -0.5),
        "lru_bi": nrm(ks[8], (DEPTH, D_LRU), 0.02),
        "lru_lambda": lam,
        "gla_w_alpha": nrm(ks[11], (DEPTH, GATE_RANK, GLA_K), GATE_RANK ** -0.5),
        "gla_b_alpha": nrm(ks[12], (DEPTH, GLA_K), 0.02),
        "gla_norm": gain(ks[13], (DEPTH, GLA_DV)),
        "w_out": nrm(ks[14], (DEPTH, D_MIX, D_MODEL), D_MIX ** -0.5),
        "norm2": gain(ks[15], (DEPTH, D_MODEL)),
        "w_ffn_in": nrm(ks[16], (DEPTH, D_MODEL, 2 * D_FF), D_MODEL ** -0.5),
        "w_ffn_out": nrm(ks[17], (DEPTH, D_FF, D_MODEL), D_FF ** -0.5),
        "final_norm": gain(ks[18], (D_MODEL,)),
    }


def reference(x, norm1, w_in, conv_w, conv_b, lru_wa, lru_ba, lru_wi, lru_bi,
              lru_lambda, gla_w_alpha, gla_b_alpha, gla_norm, w_out, norm2,
              w_ffn_in, w_ffn_out, final_norm):
    bsz, seq, _ = x.shape
    q_scale = GLA_DK ** -0.5
    for l in range(DEPTH):
        h = rms_norm(x, norm1[l])
        p = h @ w_in[l]
        lru_x, lru_g, q, k, v, g, z = jnp.split(p, SPLIT_IDX, axis=-1)

        lru_x = causal_depthwise_conv(lru_x, conv_w[l], conv_b[l])
        lru_o = rg_lru(lru_x, lru_wa[l], lru_ba[l], lru_wi[l], lru_bi[l],
                       lru_lambda[l]) * jax.nn.gelu(lru_g)

        zf = z.astype(jnp.float32) @ gla_w_alpha[l].astype(jnp.float32) + gla_b_alpha[l]
        log_alpha = jax.nn.log_sigmoid(zf) / GATE_TAU
        qh = (q * q_scale).reshape(bsz, seq, GLA_HEADS, GLA_DK)
        kh = k.reshape(bsz, seq, GLA_HEADS, GLA_DK)
        vh = v.reshape(bsz, seq, GLA_HEADS, GLA_DV)
        lah = log_alpha.reshape(bsz, seq, GLA_HEADS, GLA_DK)
        o = gla_chunk_causal(qh, kh, vh, lah)
        o = rms_norm(o, gla_norm[l]).reshape(bsz, seq, GLA_V).astype(x.dtype)
        gla_o = o * jax.nn.silu(g)

        mix = jnp.concatenate([lru_o, gla_o], axis=-1) @ w_out[l]
        x = x + mix

        h = rms_norm(x, norm2[l])
        gate, up = jnp.split(h @ w_ffn_in[l], 2, axis=-1)
        x = x + (jax.nn.silu(gate) * up) @ w_ffn_out[l]
    return rms_norm(x, final_norm)
```

```python
import functools

import jax
import jax.numpy as jnp
import numpy as np
from jax import lax
from jax.experimental import pallas as pl
from jax.experimental.pallas import tpu as pltpu

D_MODEL = 1024
SEQ = 16384
DEPTH = 4
CHUNK = 64
D_LRU = 512
LRU_BLOCKS = 8
LRU_BLOCK = 64
CONV_WIDTH = 4
LRU_C = 8.0
GLA_HEADS = 4
GLA_DV = 128
GLA_DK = 64
GLA_K = 256
GLA_V = 512
GATE_RANK = 16
GATE_TAU = 16.0
D_FF = 2816
EPS = 1e-6

LANES = 128
SUBLANES = 8
MXU_DIM = 256
Z_PAD = LANES
P_MAIN = 2 * D_LRU + 2 * GLA_K + 2 * GLA_V
P_PAD = P_MAIN + Z_PAD

OFF_LRU_X = 0
OFF_LRU_G = D_LRU
OFF_Q = 2 * D_LRU
OFF_K = OFF_Q + GLA_K
OFF_V = OFF_K + GLA_K
OFF_G = OFF_V + GLA_V
OFF_Z = OFF_G + GLA_V

SEQ_TILE = 512
FF_CHUNK = MXU_DIM
VMEM_LIMIT = 56 * 1024 * 1024

_F32 = jnp.float32
_BF16 = jnp.bfloat16


def _dot(a, b):
    return jnp.dot(a, b, preferred_element_type=_F32)


def _sigmoid(x):
    return 1.0 / (1.0 + jnp.exp(-x))


def _softplus(x):
    return jnp.maximum(x, 0.0) + jnp.log1p(jnp.exp(-jnp.abs(x)))


def _rms_scale(x):
    return lax.rsqrt(jnp.mean(x * x, axis=-1, keepdims=True) + EPS)


def _mixer_kernel(x_ref, n1_ref, win_ref, convw_ref, convb_ref, wgate_ref, bgate_ref,
                  lam_ref, walpha_ref, balpha_ref, gnorm_ref, wout_ref,
                  o_ref,
                  h_buf, conv_buf, a_buf, u_buf, mix_buf, la_buf, k_buf, q_buf, v_buf,
                  og_buf, carry_buf, st_buf):
    ts = x_ref.shape[0]
    step = pl.program_id(0)

    @pl.when(step == 0)
    def _():
        conv_buf[0:SUBLANES, :] = jnp.zeros((SUBLANES, D_LRU), _F32)
        carry_buf[...] = jnp.zeros_like(carry_buf)
        st_buf[...] = jnp.zeros_like(st_buf)

    x = x_ref[...]
    h_buf[...] = (x * _rms_scale(x) * n1_ref[...]).astype(_BF16)

    conv_buf[SUBLANES:SUBLANES + ts, :] = _dot(h_buf[...], win_ref[:, OFF_LRU_X:OFF_LRU_X + D_LRU])
    xc = convb_ref[...]
    for j in range(CONV_WIDTH):
        off = SUBLANES - (CONV_WIDTH - 1) + j
        xc = xc + conv_buf[off:off + ts, :] * convw_ref[j:j + 1, :]
    conv_buf[0:SUBLANES, :] = conv_buf[ts:ts + SUBLANES, :]

    gates = _dot(xc.astype(_BF16), wgate_ref[...]) + bgate_ref[...]
    r = _sigmoid(gates[:, :D_LRU])
    ig = _sigmoid(gates[:, D_LRU:])
    log_a = (-LRU_C) * r * _softplus(-lam_ref[...])
    a = jnp.exp(log_a)
    a_buf[...] = a
    u_buf[...] = jnp.sqrt(-jnp.tanh(log_a) * (1.0 + a * a)) * (ig * xc)

    row = lax.broadcasted_iota(jnp.int32, (SUBLANES, D_LRU), 0)

    def scan_body(g, carry):
        r0 = pl.multiple_of(g * SUBLANES, SUBLANES)
        a = a_buf[pl.ds(r0, SUBLANES), :]
        u = u_buf[pl.ds(r0, SUBLANES), :]
        for sh in (1, 2, 4):
            keep = row >= sh
            u = jnp.where(keep, u + a * pltpu.roll(u, sh, axis=0), u)
            a = jnp.where(keep, a * pltpu.roll(a, sh, axis=0), a)
        hs = u + a * carry
        u_buf[pl.ds(r0, SUBLANES), :] = hs
        return hs[SUBLANES - 1:SUBLANES, :]

    carry_buf[...] = lax.fori_loop(0, ts // SUBLANES, scan_body, carry_buf[...])

    lru_g = _dot(h_buf[...], win_ref[:, OFF_LRU_G:OFF_LRU_G + D_LRU])
    mix_buf[:, 0:D_LRU] = (u_buf[...] * jax.nn.gelu(lru_g, approximate=True)).astype(_BF16)

    q_buf[...] = (_dot(h_buf[...], win_ref[:, OFF_Q:OFF_Q + GLA_K]) * (GLA_DK ** -0.5)).astype(_BF16)
    k_buf[...] = _dot(h_buf[...], win_ref[:, OFF_K:OFF_K + GLA_K])
    v_buf[...] = _dot(h_buf[...], win_ref[:, OFF_V:OFF_V + GLA_V]).astype(_BF16)
    z = _dot(h_buf[...], win_ref[:, OFF_Z:OFF_Z + Z_PAD])
    zf = _dot(z.astype(_BF16), walpha_ref[...]) + balpha_ref[...]
    la_buf[...] = (jnp.minimum(zf, 0.0) - jnp.log1p(jnp.exp(-jnp.abs(zf)))) * (1.0 / GATE_TAU)

    tri = (lax.broadcasted_iota(jnp.int32, (CHUNK, CHUNK), 0)
           >= lax.broadcasted_iota(jnp.int32, (CHUNK, CHUNK), 1)).astype(_BF16)
    head_mask = (lax.broadcasted_iota(jnp.int32, (GLA_V, GLA_K), 0) // GLA_DV
                 == lax.broadcasted_iota(jnp.int32, (GLA_V, GLA_K), 1) // GLA_DK)
    for c in range(ts // CHUNK):
        rows = slice(c * CHUNK, (c + 1) * CHUNK)
        la = la_buf[rows, :]
        la_hi = la.astype(_BF16)
        la_lo = (la - la_hi.astype(_F32)).astype(_BF16)
        bcum = _dot(tri, la_hi) + _dot(tri, la_lo)
        bend = bcum[CHUNK - 1:CHUNK, :]
        kd = (k_buf[rows, :] * jnp.exp(bend - bcum)).astype(_BF16)
        upd_t = lax.dot_general(v_buf[rows, :], kd, (((0,), (0,)), ((), ())),
                                preferred_element_type=_F32)
        st = st_buf[...] * jnp.exp(bend) + jnp.where(head_mask, upd_t, 0.0)
        st_buf[...] = st
        og_buf[rows, :] = lax.dot_general(q_buf[rows, :], st.astype(_BF16),
                                          (((1,), (1,)), ((), ())),
                                          preferred_element_type=_F32)

    g = _dot(h_buf[...], win_ref[:, OFF_G:OFF_G + GLA_V])
    gate = g * _sigmoid(g)
    for hd in range(GLA_HEADS):
        cols = slice(hd * GLA_DV, (hd + 1) * GLA_DV)
        oh = og_buf[:, cols]
        oh = oh * _rms_scale(oh) * gnorm_ref[...]
        mix_buf[:, D_LRU + hd * GLA_DV:D_LRU + (hd + 1) * GLA_DV] = (oh * gate[:, cols]).astype(_BF16)

    o_ref[...] = x + _dot(mix_buf[...], wout_ref[...])


def _ffn_kernel(x_ref, n2_ref, w1_ref, w2_ref, fn_ref, o_ref, h_buf, act_buf, *, final):
    x = x_ref[...]
    h_buf[...] = (x * _rms_scale(x) * n2_ref[...]).astype(_BF16)
    for j in range(D_FF // FF_CHUNK):
        gate = _dot(h_buf[...], w1_ref[:, j * FF_CHUNK:(j + 1) * FF_CHUNK])
        up = _dot(h_buf[...], w1_ref[:, D_FF + j * FF_CHUNK:D_FF + (j + 1) * FF_CHUNK])
        act_buf[:, j * FF_CHUNK:(j + 1) * FF_CHUNK] = (gate * _sigmoid(gate) * up).astype(_BF16)
    y = x + _dot(act_buf[...], w2_ref[...])
    if final:
        y = y * _rms_scale(y) * fn_ref[...]
    o_ref[...] = y


def _resident(shape, layer):
    nd = len(shape)
    return pl.BlockSpec((None,) + tuple(shape[1:]), lambda i: (layer,) + (0,) * (nd - 1),
                        pipeline_mode=pl.Buffered(1))


def _row_tile():
    return pl.BlockSpec((SEQ_TILE, D_MODEL), lambda i: (i, 0))


def _mixer_call(x, layer, n1, win, convw, convb, wgate, bgate, lam, walpha, balpha, gnorm, wout):
    ts = SEQ_TILE
    params = (n1, win, convw, convb, wgate, bgate, lam, walpha, balpha, gnorm, wout)
    return pl.pallas_call(
        _mixer_kernel,
        grid=(SEQ // ts,),
        in_specs=[_row_tile()] + [_resident(p.shape, layer) for p in params],
        out_specs=_row_tile(),
        out_shape=jax.ShapeDtypeStruct((SEQ, D_MODEL), _F32),
        scratch_shapes=[
            pltpu.VMEM((ts, D_MODEL), _BF16),
            pltpu.VMEM((ts + SUBLANES, D_LRU), _F32),
            pltpu.VMEM((ts, D_LRU), _F32),
            pltpu.VMEM((ts, D_LRU), _F32),
            pltpu.VMEM((ts, D_MODEL), _BF16),
            pltpu.VMEM((ts, GLA_K), _F32),
            pltpu.VMEM((ts, GLA_K), _F32),
            pltpu.VMEM((ts, GLA_K), _BF16),
            pltpu.VMEM((ts, GLA_V), _BF16),
            pltpu.VMEM((ts, GLA_V), _F32),
            pltpu.VMEM((1, D_LRU), _F32),
            pltpu.VMEM((GLA_V, GLA_K), _F32),
        ],
        compiler_params=pltpu.CompilerParams(
            dimension_semantics=("arbitrary",), vmem_limit_bytes=VMEM_LIMIT),
        name=f"mixer_l{layer}",
    )(x, *params)


def _ffn_call(x, layer, n2, w1, w2, fnorm, final):
    ts = SEQ_TILE
    return pl.pallas_call(
        functools.partial(_ffn_kernel, final=final),
        grid=(SEQ // ts,),
        in_specs=[_row_tile(), _resident(n2.shape, layer), _resident(w1.shape, layer),
                  _resident(w2.shape, layer),
                  pl.BlockSpec((1, D_MODEL), lambda i: (0, 0))],
        out_specs=_row_tile(),
        out_shape=jax.ShapeDtypeStruct((SEQ, D_MODEL), _F32),
        scratch_shapes=[
            pltpu.VMEM((ts, D_MODEL), _BF16),
            pltpu.VMEM((ts, D_FF), _BF16),
        ],
        compiler_params=pltpu.CompilerParams(
            dimension_semantics=("arbitrary",), vmem_limit_bytes=VMEM_LIMIT),
        name=f"ffn_l{layer}",
    )(x, n2, w1, w2, fnorm)


def _block_diag(w):
    eye = jnp.eye(LRU_BLOCKS, dtype=w.dtype)
    dense = w[:, :, :, None, :] * eye[None, :, None, :, None]
    return dense.reshape(DEPTH, D_LRU, D_LRU)


def kernel(x, norm1, w_in, conv_w, conv_b, lru_wa, lru_ba, lru_wi, lru_bi, lru_lambda,
           gla_w_alpha, gla_b_alpha, gla_norm, w_out, norm2, w_ffn_in, w_ffn_out, final_norm):
    assert x.shape == (1, SEQ, D_MODEL)
    row3 = lambda p: p.reshape(DEPTH, 1, -1).astype(_F32)
    win = jnp.pad(w_in, ((0, 0), (0, 0), (0, P_PAD - w_in.shape[-1]))).astype(_BF16)
    wgate = jnp.concatenate([_block_diag(lru_wa), _block_diag(lru_wi)], axis=-1).astype(_BF16)
    bgate = jnp.concatenate([lru_ba, lru_bi], axis=-1).reshape(DEPTH, 1, 2 * D_LRU)
    walpha = jnp.pad(gla_w_alpha, ((0, 0), (0, Z_PAD - GATE_RANK), (0, 0))).astype(_BF16)
    wout = w_out.astype(_BF16)
    w1 = w_ffn_in.astype(_BF16)
    w2 = w_ffn_out.astype(_BF16)
    fnorm = final_norm.reshape(1, D_MODEL)

    y = x.reshape(SEQ, D_MODEL)
    for layer in range(DEPTH):
        y = _mixer_call(y, layer, row3(norm1), win, conv_w, row3(conv_b), wgate, bgate,
                        row3(lru_lambda), walpha, row3(gla_b_alpha), row3(gla_norm), wout)
        y = _ffn_call(y, layer, row3(norm2), w1, w2, fnorm, final=(layer == DEPTH - 1))
    return y.reshape(1, SEQ, D_MODEL)
```

```python
import functools
import math

import jax
import jax.numpy as jnp
import numpy as np
from jax import lax
from jax.experimental import pallas as pl
from jax.experimental.pallas import tpu as pltpu

D_MODEL = 1024
SEQ = 16384
DEPTH = 4
CHUNK = 64
D_LRU = 512
LRU_BLOCKS = 8
LRU_BLOCK = 64
CONV_WIDTH = 4
LRU_C = 8.0
GLA_HEADS = 4
GLA_DV = 128
GLA_DK = 64
GLA_K = 256
GLA_V = 512
GATE_RANK = 16
GATE_TAU = 16.0
D_FF = 2816
EPS = 1e-6

LANES = 128
SUBLANES = 8
MXU_DIM = 256
Z_PAD = LANES
P_MAIN = 2 * D_LRU + 2 * GLA_K + 2 * GLA_V
P_PAD = P_MAIN + Z_PAD

OFF_LRU_X = 0
OFF_LRU_G = D_LRU
OFF_Q = 2 * D_LRU
OFF_K = OFF_Q + GLA_K
OFF_V = OFF_K + GLA_K
OFF_G = OFF_V + GLA_V
OFF_Z = OFF_G + GLA_V

SEQ_TILE = 512
SEG_PITCH = SEQ_TILE // SUBLANES + 4
GATE_GROUP = MXU_DIM
FF_CHUNK = MXU_DIM
VMEM_LIMIT = 56 * 1024 * 1024

_F32 = jnp.float32
_BF16 = jnp.bfloat16


def _dot(a, b):
    return jnp.dot(a, b, preferred_element_type=_F32)


def _sigmoid(x):
    return 1.0 / (1.0 + jnp.exp(-x))


def _softplus(x):
    return jnp.maximum(x, 0.0) + jnp.log1p(jnp.exp(-jnp.abs(x)))


def _gelu_tanh(x):
    c = math.sqrt(2.0 / math.pi)
    return x * (0.5 + 0.5 * jnp.tanh(x * (c + (c * 0.044715) * (x * x))))


def _rms_scale(x):
    return lax.rsqrt(jnp.mean(x * x, axis=-1, keepdims=True) + EPS)


def _mixer_kernel(x_ref, n1_ref, win_ref, convw_ref, convb_ref, wgate_ref, bgate_ref,
                  lam_ref, walpha_ref, balpha_ref, gnorm_ref, wout_ref, tri_ref,
                  o_ref,
                  h_buf, xpad, xprev, a_buf, u_buf, hl_buf, pc_buf, hpad, lg_buf, mix_buf,
                  la_buf, k_buf, qm_buf, kdm_buf, vs_buf, g_buf, upd_buf, stb_buf, og_buf,
                  carry_buf, st_buf):
    ts = x_ref.shape[0]
    seg = ts // SUBLANES
    nlt = D_LRU // LANES
    nchunk = ts // CHUNK
    step = pl.program_id(0)

    @pl.when(step == 0)
    def _():
        xprev[...] = jnp.zeros_like(xprev)
        carry_buf[...] = jnp.zeros_like(carry_buf)
        st_buf[...] = jnp.zeros_like(st_buf)

    x = x_ref[...]
    h_buf[...] = (x * _rms_scale(x) * n1_ref[...]).astype(_BF16)

    lx = _dot(h_buf[...], win_ref[:, OFF_LRU_X:OFF_LRU_X + D_LRU])
    for lt in range(nlt):
        for s in range(SUBLANES):
            xpad[lt, s * SEG_PITCH:s * SEG_PITCH + seg, :] = lx[s * seg:(s + 1) * seg, lt * LANES:(lt + 1) * LANES]
    z = _dot(h_buf[...], win_ref[:, OFF_Z:OFF_Z + Z_PAD])
    zf = _dot(z.astype(_BF16), walpha_ref[...]) + balpha_ref[...]
    la_buf[...] = (jnp.minimum(zf, 0.0) - jnp.log(1.0 + jnp.exp(-jnp.abs(zf)))) * (1.0 / GATE_TAU)
    k_buf[...] = _dot(h_buf[...], win_ref[:, OFF_K:OFF_K + GLA_K])
    v = _dot(h_buf[...], win_ref[:, OFF_V:OFF_V + GLA_V]).astype(_BF16)
    for c in range(nchunk):
        for hd in range(GLA_HEADS):
            r0 = (c * GLA_HEADS + hd) * CHUNK
            vs_buf[r0:r0 + CHUNK, :] = v[c * CHUNK:(c + 1) * CHUNK, hd * GLA_DV:(hd + 1) * GLA_DV]
    q = _dot(h_buf[...], win_ref[:, OFF_Q:OFF_Q + GLA_K]) * (GLA_DK ** -0.5)
    lane_k = lax.broadcasted_iota(jnp.int32, (CHUNK, GLA_K), 1)
    for c in range(nchunk):
        qc = q[c * CHUNK:(c + 1) * CHUNK, :]
        for hd in range(GLA_HEADS):
            r0 = (c * GLA_HEADS + hd) * CHUNK
            qm_buf[r0:r0 + CHUNK, :] = jnp.where(lane_k // GLA_DK == hd, qc, 0.0).astype(_BF16)
    lg_buf[...] = _gelu_tanh(_dot(h_buf[...], win_ref[:, OFF_LRU_G:OFF_LRU_G + D_LRU]))
    g = _dot(h_buf[...], win_ref[:, OFF_G:OFF_G + GLA_V])
    g_buf[...] = g * _sigmoid(g)

    row = lax.broadcasted_iota(jnp.int32, (SUBLANES, D_LRU), 0)
    xp = [jnp.concatenate([xpad[lt, pl.ds(j, SUBLANES, stride=SEG_PITCH), :] for lt in range(nlt)], axis=1)
          for j in range(seg)]
    ctx = [pltpu.roll(jnp.where(row == SUBLANES - 1, xprev[m], xp[seg - (CONV_WIDTH - 1) + m]), 1, axis=0)
           for m in range(CONV_WIDTH - 1)]
    for m in range(CONV_WIDTH - 1):
        xprev[m] = xp[seg - (CONV_WIDTH - 1) + m]
    xseq = jnp.concatenate(ctx + xp, axis=0)
    xc = convb_ref[...]
    for j in range(CONV_WIDTH):
        xc = xc + xseq[j * SUBLANES:j * SUBLANES + ts, :] * convw_ref[j:j + 1, :]

    sp_lam = _softplus(-lam_ref[...])
    for hf in range(D_LRU // GATE_GROUP):
        cols = slice(hf * GATE_GROUP, (hf + 1) * GATE_GROUP)
        xh = xc[:, cols]
        gates = _dot(xh.astype(_BF16), wgate_ref[hf]) + bgate_ref[hf]
        r = _sigmoid(gates[:, :GATE_GROUP])
        ig = _sigmoid(gates[:, GATE_GROUP:])
        log_a = (-LRU_C) * r * sp_lam[:, cols]
        a = jnp.exp(log_a)
        a_buf[:, cols] = a
        u_buf[:, cols] = jnp.sqrt(-jnp.tanh(log_a) * (1.0 + a * a)) * (ig * xh)

    hloc = jnp.zeros((SUBLANES, D_LRU), _F32)
    pcum = jnp.ones((SUBLANES, D_LRU), _F32)
    for j in range(seg):
        rows = slice(j * SUBLANES, (j + 1) * SUBLANES)
        aj = a_buf[rows, :]
        hloc = aj * hloc + u_buf[rows, :]
        pcum = pcum * aj
        hl_buf[rows, :] = hloc
        pc_buf[rows, :] = pcum
    cin = carry_buf[...]
    cins = []
    for s in range(SUBLANES):
        cins.append(cin)
        cin = pcum[s:s + 1, :] * cin + hloc[s:s + 1, :]
    carry_buf[...] = cin
    cmat = jnp.broadcast_to(cins[SUBLANES - 1], (SUBLANES, D_LRU))
    for s in range(SUBLANES - 2, -1, -1):
        cmat = jnp.where(row == s, jnp.broadcast_to(cins[s], (SUBLANES, D_LRU)), cmat)
    for j in range(seg):
        rows = slice(j * SUBLANES, (j + 1) * SUBLANES)
        hs = hl_buf[rows, :] + pc_buf[rows, :] * cmat
        for lt in range(nlt):
            hpad[lt, pl.ds(j, SUBLANES, stride=SEG_PITCH), :] = hs[:, lt * LANES:(lt + 1) * LANES]
    for lt in range(nlt):
        hn = jnp.concatenate([hpad[lt, s * SEG_PITCH:s * SEG_PITCH + seg, :] for s in range(SUBLANES)], axis=0)
        lcols = slice(lt * LANES, (lt + 1) * LANES)
        mix_buf[:, lcols] = (hn * lg_buf[:, lcols]).astype(_BF16)

    la = la_buf[...]
    la_hi = la.astype(_BF16)
    la_lo = (la - la_hi.astype(_F32)).astype(_BF16)
    bcum = _dot(tri_ref[...], la_hi) + _dot(tri_ref[...], la_lo)
    bcum3 = bcum.reshape(nchunk, CHUNK, GLA_K)
    bend = bcum3[:, CHUNK - 1:CHUNK, :]
    kd = k_buf[...] * jnp.exp(jnp.broadcast_to(bend, bcum3.shape) - bcum3).reshape(ts, GLA_K)
    dec = jnp.exp(bend.reshape(nchunk, GLA_K))
    for c in range(nchunk):
        kc = kd[c * CHUNK:(c + 1) * CHUNK, :]
        for hd in range(GLA_HEADS):
            r0 = (c * GLA_HEADS + hd) * CHUNK
            kdm_buf[r0:r0 + CHUNK, :] = jnp.where(lane_k // GLA_DK == hd, kc, 0.0).astype(_BF16)
    hrows = GLA_HEADS * CHUNK
    for c in range(nchunk):
        upd_buf[c] = lax.dot_general(vs_buf[c * hrows:(c + 1) * hrows, :], kdm_buf[c * hrows:(c + 1) * hrows, :],
                                     (((0,), (0,)), ((), ())), preferred_element_type=_F32)
    st = st_buf[...]
    for c in range(nchunk):
        st = st * dec[c:c + 1, :] + upd_buf[c]
        stb_buf[c] = st.astype(_BF16)
    st_buf[...] = st
    for c in range(nchunk):
        o4 = lax.dot_general(qm_buf[c * hrows:(c + 1) * hrows, :], stb_buf[c],
                             (((1,), (1,)), ((), ())), preferred_element_type=_F32)
        for hd in range(GLA_HEADS):
            og_buf[c * CHUNK:(c + 1) * CHUNK, hd * GLA_DV:(hd + 1) * GLA_DV] = o4[hd * CHUNK:(hd + 1) * CHUNK, :]

    for hd in range(GLA_HEADS):
        cols = slice(hd * GLA_DV, (hd + 1) * GLA_DV)
        oh = og_buf[:, cols]
        oh = oh * _rms_scale(oh) * gnorm_ref[...]
        mix_buf[:, D_LRU + hd * GLA_DV:D_LRU + (hd + 1) * GLA_DV] = (oh * g_buf[:, cols]).astype(_BF16)

    o_ref[...] = x + _dot(mix_buf[...], wout_ref[...])


def _ffn_kernel(x_ref, n2_ref, w1_ref, w2_ref, fn_ref, o_ref, h_buf, act_buf, *, final):
    x = x_ref[...]
    h_buf[...] = (x * _rms_scale(x) * n2_ref[...]).astype(_BF16)
    for j in range(D_FF // FF_CHUNK):
        gate = _dot(h_buf[...], w1_ref[:, j * FF_CHUNK:(j + 1) * FF_CHUNK])
        up = _dot(h_buf[...], w1_ref[:, D_FF + j * FF_CHUNK:D_FF + (j + 1) * FF_CHUNK])
        act_buf[:, j * FF_CHUNK:(j + 1) * FF_CHUNK] = (gate * _sigmoid(gate) * up).astype(_BF16)
    y = x + _dot(act_buf[...], w2_ref[...])
    if final:
        y = y * _rms_scale(y) * fn_ref[...]
    o_ref[...] = y


def _resident(shape, layer):
    nd = len(shape)
    return pl.BlockSpec((None,) + tuple(shape[1:]), lambda i: (layer,) + (0,) * (nd - 1),
                        pipeline_mode=pl.Buffered(1))


def _row_tile():
    return pl.BlockSpec((SEQ_TILE, D_MODEL), lambda i: (i, 0))


def _mixer_call(x, layer, n1, win, convw, convb, wgate, bgate, lam, walpha, balpha, gnorm, wout, tri):
    ts = SEQ_TILE
    params = (n1, win, convw, convb, wgate, bgate, lam, walpha, balpha, gnorm, wout)
    return pl.pallas_call(
        _mixer_kernel,
        grid=(SEQ // ts,),
        in_specs=[_row_tile()] + [_resident(p.shape, layer) for p in params]
        + [pl.BlockSpec(tri.shape, lambda i: (0, 0), pipeline_mode=pl.Buffered(1))],
        out_specs=_row_tile(),
        out_shape=jax.ShapeDtypeStruct((SEQ, D_MODEL), _F32),
        scratch_shapes=[
            pltpu.VMEM((ts, D_MODEL), _BF16),
            pltpu.VMEM((D_LRU // LANES, SUBLANES * SEG_PITCH, LANES), _F32),
            pltpu.VMEM((CONV_WIDTH - 1, SUBLANES, D_LRU), _F32),
            pltpu.VMEM((ts, D_LRU), _F32),
            pltpu.VMEM((ts, D_LRU), _F32),
            pltpu.VMEM((ts, D_LRU), _F32),
            pltpu.VMEM((ts, D_LRU), _F32),
            pltpu.VMEM((D_LRU // LANES, SUBLANES * SEG_PITCH, LANES), _F32),
            pltpu.VMEM((ts, D_LRU), _F32),
            pltpu.VMEM((ts, D_MODEL), _BF16),
            pltpu.VMEM((ts, GLA_K), _F32),
            pltpu.VMEM((ts, GLA_K), _F32),
            pltpu.VMEM((ts * GLA_HEADS, GLA_K), _BF16),
            pltpu.VMEM((ts * GLA_HEADS, GLA_K), _BF16),
            pltpu.VMEM((ts * GLA_HEADS, GLA_DV), _BF16),
            pltpu.VMEM((ts, GLA_V), _F32),
            pltpu.VMEM((ts // CHUNK, GLA_DV, GLA_K), _F32),
            pltpu.VMEM((ts // CHUNK, GLA_DV, GLA_K), _BF16),
            pltpu.VMEM((ts, GLA_V), _F32),
            pltpu.VMEM((1, D_LRU), _F32),
            pltpu.VMEM((GLA_DV, GLA_K), _F32),
        ],
        compiler_params=pltpu.CompilerParams(
            dimension_semantics=("arbitrary",), vmem_limit_bytes=VMEM_LIMIT),
        name=f"mixer_l{layer}",
    )(x, *params, tri)


def _ffn_call(x, layer, n2, w1, w2, fnorm, final):
    ts = SEQ_TILE
    return pl.pallas_call(
        functools.partial(_ffn_kernel, final=final),
        grid=(SEQ // ts,),
        in_specs=[_row_tile(), _resident(n2.shape, layer), _resident(w1.shape, layer),
                  _resident(w2.shape, layer),
                  pl.BlockSpec((1, D_MODEL), lambda i: (0, 0))],
        out_specs=_row_tile(),
        out_shape=jax.ShapeDtypeStruct((SEQ, D_MODEL), _F32),
        scratch_shapes=[
            pltpu.VMEM((ts, D_MODEL), _BF16),
            pltpu.VMEM((ts, D_FF), _BF16),
        ],
        compiler_params=pltpu.CompilerParams(
            dimension_semantics=("arbitrary",), vmem_limit_bytes=VMEM_LIMIT),
        name=f"ffn_l{layer}",
    )(x, n2, w1, w2, fnorm)


def _gate_weights(wa, wi):
    per = GATE_GROUP // LRU_BLOCK
    ngrp = D_LRU // GATE_GROUP
    eye = jnp.eye(per, dtype=wa.dtype)

    def dense(w):
        w = w.reshape(DEPTH, ngrp, per, LRU_BLOCK, LRU_BLOCK)
        d = w[:, :, :, :, None, :] * eye[None, None, :, None, :, None]
        return d.reshape(DEPTH, ngrp, GATE_GROUP, GATE_GROUP)

    return jnp.concatenate([dense(wa), dense(wi)], axis=-1)


def kernel(x, norm1, w_in, conv_w, conv_b, lru_wa, lru_ba, lru_wi, lru_bi, lru_lambda,
           gla_w_alpha, gla_b_alpha, gla_norm, w_out, norm2, w_ffn_in, w_ffn_out, final_norm):
    assert x.shape == (1, SEQ, D_MODEL)
    row3 = lambda p: p.reshape(DEPTH, 1, -1).astype(_F32)
    win = jnp.pad(w_in, ((0, 0), (0, 0), (0, P_PAD - w_in.shape[-1]))).astype(_BF16)
    ngrp = D_LRU // GATE_GROUP
    wgate = _gate_weights(lru_wa, lru_wi).astype(_BF16)
    bgate = jnp.concatenate([lru_ba.reshape(DEPTH, ngrp, 1, GATE_GROUP),
                             lru_bi.reshape(DEPTH, ngrp, 1, GATE_GROUP)], axis=-1)
    walpha = jnp.pad(gla_w_alpha, ((0, 0), (0, Z_PAD - GATE_RANK), (0, 0))).astype(_BF16)
    wout = w_out.astype(_BF16)
    w1 = w_ffn_in.astype(_BF16)
    w2 = w_ffn_out.astype(_BF16)
    fnorm = final_norm.reshape(1, D_MODEL)

    t_idx = np.arange(SEQ_TILE)
    tri = jnp.asarray((t_idx[:, None] >= t_idx[None, :]) & (t_idx[:, None] // CHUNK == t_idx[None, :] // CHUNK), _BF16)

    y = x.reshape(SEQ, D_MODEL)
    for layer in range(DEPTH):
        y = _mixer_call(y, layer, row3(norm1), win, conv_w, row3(conv_b), wgate, bgate,
                        row3(lru_lambda), walpha, row3(gla_b_alpha), row3(gla_norm), wout, tri)
        y = _ffn_call(y, layer, row3(norm2), w1, w2, fnorm, final=(layer == DEPTH - 1))
    return y.reshape(1, SEQ, D_MODEL)
```
